```python
import math
import jax, jax.numpy as jnp
from jax import lax
import numpy as np

D_MODEL = 2048
BATCH = 1
SEQ = 8192
DEPTH = 1

POOL_WINDOWS = (2, 4, 8, 16)
N_POOL_GROUPS = len(POOL_WINDOWS)
POOL_WIDTH = D_MODEL // 2
POOL_GROUP_WIDTH = POOL_WIDTH // N_POOL_GROUPS
POOL_OUT_GROUP = D_MODEL // N_POOL_GROUPS
GMLP_HEADS = 8
GMLP_HEAD_DIM = 128
GMLP_WIDTH = GMLP_HEADS * GMLP_HEAD_DIM
CHUNK = 128
IN_WIDTH = POOL_WIDTH + 2 * GMLP_WIDTH + 2 * D_MODEL
PEER_HEADS = 8
N_KEYS = 128
N_EXPERTS = N_KEYS * N_KEYS
QUERY_DIM = 256
HALF_DIM = QUERY_DIM // 2
TOPK = 16
TOKEN_BLOCK = 128
EPS = 1e-6

kernel_name = "hybrid_pool_sgu_peer_block"


def rms_norm(x, g):
    xf = x.astype(jnp.float32)
    y = xf * lax.rsqrt(jnp.mean(xf * xf, axis=-1, keepdims=True) + EPS)
    return (y * g.astype(jnp.float32)).astype(x.dtype)


def layer_norm(x, g, b):
    xf = x.astype(jnp.float32)
    mu = jnp.mean(xf, axis=-1, keepdims=True)
    var = jnp.mean(jnp.square(xf - mu), axis=-1, keepdims=True)
    y = (xf - mu) * lax.rsqrt(var + EPS)
    return (y * g.astype(jnp.float32) + b.astype(jnp.float32)).astype(x.dtype)


def pool_mixer(a, pool_scale, w_pool):
    B, S, _ = a.shape
    af = a.astype(jnp.float32)
    csum = jnp.cumsum(af, axis=1)
    pos1 = jnp.arange(1, S + 1, dtype=jnp.float32)
    groups = []
    for gi, w in enumerate(POOL_WINDOWS):
        sl = slice(gi * POOL_GROUP_WIDTH, (gi + 1) * POOL_GROUP_WIDTH)
        cg = csum[..., sl]
        prev = jnp.pad(cg, ((0, 0), (w, 0), (0, 0)))[:, :S]
        cnt = jnp.minimum(pos1, float(w))[None, :, None]
        groups.append((cg - prev) / cnt - af[..., sl])
    pooled = jnp.concatenate(groups, axis=-1) * pool_scale.astype(jnp.float32)
    pooled = pooled.astype(a.dtype).reshape(B, S, N_POOL_GROUPS, POOL_GROUP_WIDTH)
    y = jnp.einsum('bsgc,gcd->bsgd', pooled, w_pool)
    return y.reshape(B, S, D_MODEL)


def sgu_mixer(u, v, ln_v_g, ln_v_b, w_s, b_s, w_gproj):
    B, S, _ = u.shape
    u = jax.nn.gelu(u)
    v = layer_norm(jax.nn.gelu(v), ln_v_g, ln_v_b)
    nc = S // CHUNK
    v = v.reshape(B, nc, CHUNK, GMLP_HEADS, GMLP_HEAD_DIM)
    mask = jnp.tril(jnp.ones((CHUNK, CHUNK), dtype=w_s.dtype))
    sv = jnp.einsum('hts,bnshd->bnthd', w_s * mask[None], v)
    sv = sv + jnp.transpose(b_s)[None, None, :, :, None]
    y = u.reshape(B, nc, CHUNK, GMLP_HEADS, GMLP_HEAD_DIM) * sv
    return y.reshape(B, S, GMLP_WIDTH) @ w_gproj


def peer_ffn(h, w_q, sub_keys, expert_u, expert_v):
    B, S, D = h.shape
    q = (h @ w_q).astype(jnp.float32).reshape(B, S, PEER_HEADS, 2, HALF_DIM)
    kf = sub_keys.astype(jnp.float32)
    s1 = jnp.einsum('bshk,hnk->bshn', q[..., 0, :], kf[:, 0])
    s2 = jnp.einsum('bshk,hnk->bshn', q[..., 1, :], kf[:, 1])
    t1, i1 = lax.top_k(s1, TOPK)
    t2, i2 = lax.top_k(s2, TOPK)
    cand = (t1[..., :, None] + t2[..., None, :]).reshape(B, S, PEER_HEADS, TOPK * TOPK)
    cand_idx = (i1[..., :, None] * N_KEYS + i2[..., None, :]).reshape(B, S, PEER_HEADS, TOPK * TOPK)
    ts, ti = lax.top_k(cand, TOPK)
    idx = jnp.take_along_axis(cand_idx, ti, axis=-1)
    gate = jax.nn.softmax(ts, axis=-1)
    T = B * S
    E = PEER_HEADS * TOPK
    nb = T // TOKEN_BLOCK
    xb = h.reshape(nb, TOKEN_BLOCK, D)
    ib = idx.reshape(nb, TOKEN_BLOCK, E)
    gb = gate.reshape(nb, TOKEN_BLOCK, E).astype(h.dtype)

    def block(args):
        xt, it, gt = args
        ue = expert_u[it]
        ve = expert_v[it]
        act = jax.nn.gelu(jnp.einsum('td,ted->te', xt, ue))
        return jnp.einsum('te,ted->td', gt * act, ve)

    y = lax.map(block, (xb, ib, gb))
    return y.reshape(B, S, D)


def setup_inputs(seed: int = 0) -> dict:
    key = jax.random.key(seed)
    ks = jax.random.split(key, 20)
    f32 = jnp.float32
    nrm = lambda k, shape, s: jax.random.normal(k, shape, f32) * s
    x = nrm(ks[0], (BATCH, SEQ, D_MODEL), 1.0)
    norm1_g = 1.0 + nrm(ks[1], (D_MODEL,), 0.02)
    w_in = nrm(ks[2], (D_MODEL, IN_WIDTH), D_MODEL ** -0.5)
    b_gate = nrm(ks[3], (2 * D_MODEL,), 0.02)
    pool_scale = 1.0 + nrm(ks[4], (POOL_WIDTH,), 0.02)
    w_pool = nrm(ks[5], (N_POOL_GROUPS, POOL_GROUP_WIDTH, POOL_OUT_GROUP), POOL_GROUP_WIDTH ** -0.5)
    ln_v_g = 1.0 + nrm(ks[6], (GMLP_WIDTH,), 0.02)
    ln_v_b = nrm(ks[7], (GMLP_WIDTH,), 0.02)
    w_s = nrm(ks[8], (GMLP_HEADS, CHUNK, CHUNK), 0.5 * CHUNK ** -0.5)
    b_s = 1.0 + nrm(ks[9], (GMLP_HEADS, CHUNK), 0.02)
    w_gproj = nrm(ks[10], (GMLP_WIDTH, D_MODEL), GMLP_WIDTH ** -0.5)
    w_out = nrm(ks[11], (D_MODEL, D_MODEL), D_MODEL ** -0.5)
    norm2_g = 1.0 + nrm(ks[12], (D_MODEL,), 0.02)
    w_q = nrm(ks[13], (D_MODEL, PEER_HEADS * QUERY_DIM), D_MODEL ** -0.5)
    sub_keys = nrm(ks[14], (PEER_HEADS, 2, N_KEYS, HALF_DIM), HALF_DIM ** -0.5)
    expert_u = nrm(ks[15], (N_EXPERTS, D_MODEL), D_MODEL ** -0.5)
    expert_v = nrm(ks[16], (N_EXPERTS, D_MODEL), PEER_HEADS ** -0.5)
    final_g = 1.0 + nrm(ks[17], (D_MODEL,), 0.02)
    return {"x": x, "norm1_g": norm1_g, "w_in": w_in, "b_gate": b_gate,
            "pool_scale": pool_scale, "w_pool": w_pool, "ln_v_g": ln_v_g, "ln_v_b": ln_v_b,
            "w_s": w_s, "b_s": b_s, "w_gproj": w_gproj, "w_out": w_out,
            "norm2_g": norm2_g, "w_q": w_q, "sub_keys": sub_keys,
            "expert_u": expert_u, "expert_v": expert_v, "final_g": final_g}


def reference(x, norm1_g, w_in, b_gate, pool_scale, w_pool, ln_v_g, ln_v_b,
              w_s, b_s, w_gproj, w_out, norm2_g, w_q, sub_keys,
              expert_u, expert_v, final_g):
    P, G, D = POOL_WIDTH, GMLP_WIDTH, D_MODEL
    for _ in range(DEPTH):
        h = rms_norm(x, norm1_g)
        z = h @ w_in
        y_pool = pool_mixer(z[..., :P], pool_scale, w_pool)
        y_sgu = sgu_mixer(z[..., P:P + G], z[..., P + G:P + 2 * G],
                          ln_v_g, ln_v_b, w_s, b_s, w_gproj)
        gates = jax.nn.sigmoid(z[..., P + 2 * G:] + b_gate)
        merged = gates[..., :D] * y_pool + gates[..., D:] * y_sgu
        x = x + merged @ w_out
        x = x + peer_ffn(rms_norm(x, norm2_g), w_q, sub_keys, expert_u, expert_v)
    return rms_norm(x, final_g)
```

```python
import functools

import jax
import jax.numpy as jnp
from jax import lax
from jax.experimental import pallas as pl
from jax.experimental.pallas import tpu as pltpu

D_MODEL = 2048
SEQ = 8192
POOL_WINDOWS = (2, 4, 8, 16)
POOL_WIDTH = 1024
POOL_GROUP_WIDTH = 256
POOL_OUT_GROUP = 512
GMLP_HEADS = 8
GMLP_HEAD_DIM = 128
GMLP_WIDTH = 1024
CHUNK = 128
IN_WIDTH = POOL_WIDTH + 2 * GMLP_WIDTH + 2 * D_MODEL
PEER_HEADS = 8
N_KEYS = 128
N_EXPERTS = N_KEYS * N_KEYS
HALF_DIM = 128
QUERY_DIM = 256
TOPK = 16
EPS = 1e-6

LANES = 128
MAX_WINDOW = max(POOL_WINDOWS)
NEG = -1e30
VMEM_LIMIT = 56 * 1024 * 1024

F32 = jnp.float32
BF16 = jnp.bfloat16


def _rms(x, g):
    return x * lax.rsqrt(jnp.mean(x * x, axis=-1, keepdims=True) + EPS) * g


def _in_proj_kernel(x_ref, g_ref, w_ref, z_ref, h_ref):
    @pl.when(pl.program_id(1) == 0)
    def _():
        h_ref[...] = _rms(x_ref[...], g_ref[...]).astype(BF16)

    z_ref[...] = jnp.dot(h_ref[...], w_ref[...], preferred_element_type=F32)


def _in_proj(x, g, w_in_bf, tm=1024, tn=1024):
    return pl.pallas_call(
        _in_proj_kernel,
        grid=(SEQ // tm, IN_WIDTH // tn),
        in_specs=[
            pl.BlockSpec((tm, D_MODEL), lambda i, j: (i, 0)),
            pl.BlockSpec((1, D_MODEL), lambda i, j: (0, 0)),
            pl.BlockSpec((D_MODEL, tn), lambda i, j: (0, j)),
        ],
        out_specs=pl.BlockSpec((tm, tn), lambda i, j: (i, j)),
        out_shape=jax.ShapeDtypeStruct((SEQ, IN_WIDTH), F32),
        scratch_shapes=[pltpu.VMEM((tm, D_MODEL), BF16)],
        compiler_params=pltpu.CompilerParams(
            dimension_semantics=("arbitrary", "arbitrary"), vmem_limit_bytes=VMEM_LIMIT),
        name="in_proj",
    )(x, g, w_in_bf)


def _mixer_kernel(z_ref, x_ref, bg_ref, ps_ref, wp_ref, lng_ref, lnb_ref, ws_ref, bst_ref,
                  wg_ref, wo_ref, o_ref, halo_ref, ext_ref, ysgu_ref, ypool_ref, *, tm):
    i = pl.program_id(0)

    @pl.when(i == 0)
    def _():
        halo_ref[...] = jnp.zeros_like(halo_ref)

    a = z_ref[:, 0:POOL_WIDTH]
    ext_ref[0:MAX_WINDOW, :] = halo_ref[...]
    ext_ref[MAX_WINDOW:MAX_WINDOW + tm, :] = a
    halo_ref[...] = a[tm - MAX_WINDOW:tm, :]
    pos1 = (i * tm + 1 + lax.broadcasted_iota(jnp.int32, (tm, POOL_GROUP_WIDTH), 0)).astype(F32)
    for gi, w in enumerate(POOL_WINDOWS):
        lo = gi * POOL_GROUP_WIDTH
        hi = lo + POOL_GROUP_WIDTH
        win = ext_ref[MAX_WINDOW:MAX_WINDOW + tm, lo:hi]
        for k in range(1, w):
            win = win + ext_ref[MAX_WINDOW - k:MAX_WINDOW - k + tm, lo:hi]
        cnt = jnp.minimum(pos1, float(w))
        pooled = (win / cnt - ext_ref[MAX_WINDOW:MAX_WINDOW + tm, lo:hi]) * ps_ref[:, lo:hi]
        ypool_ref[:, gi * POOL_OUT_GROUP:(gi + 1) * POOL_OUT_GROUP] = jnp.dot(
            pooled.astype(BF16), wp_ref[gi], preferred_element_type=F32)

    u = jax.nn.gelu(z_ref[:, POOL_WIDTH:POOL_WIDTH + GMLP_WIDTH])
    vg = jax.nn.gelu(z_ref[:, POOL_WIDTH + GMLP_WIDTH:POOL_WIDTH + 2 * GMLP_WIDTH])
    mu = jnp.mean(vg, axis=-1, keepdims=True)
    vc = vg - mu
    var = jnp.mean(vc * vc, axis=-1, keepdims=True)
    vn = (vc * lax.rsqrt(var + EPS) * lng_ref[...] + lnb_ref[...]).astype(BF16)
    row = lax.broadcasted_iota(jnp.int32, (CHUNK, CHUNK), 0)
    col = lax.broadcasted_iota(jnp.int32, (CHUNK, CHUNK), 1)
    for h in range(GMLP_HEADS):
        wsm = jnp.where(row >= col, ws_ref[h], 0.0).astype(BF16)
        bias = bst_ref[:, h:h + 1]
        cs = slice(h * GMLP_HEAD_DIM, (h + 1) * GMLP_HEAD_DIM)
        for c in range(tm // CHUNK):
            rs = slice(c * CHUNK, (c + 1) * CHUNK)
            sv = jnp.dot(wsm, vn[rs, cs], preferred_element_type=F32) + bias
            ysgu_ref[rs, cs] = (u[rs, cs] * sv).astype(BF16)
    y_sgu = jnp.dot(ysgu_ref[...], wg_ref[...], preferred_element_type=F32)

    g_off = POOL_WIDTH + 2 * GMLP_WIDTH
    ga = jax.nn.sigmoid(z_ref[:, g_off:g_off + D_MODEL] + bg_ref[:, 0:D_MODEL])
    gb = jax.nn.sigmoid(z_ref[:, g_off + D_MODEL:g_off + 2 * D_MODEL] + bg_ref[:, D_MODEL:2 * D_MODEL])
    merged = (ga * ypool_ref[...] + gb * y_sgu).astype(BF16)
    o_ref[...] = x_ref[...] + jnp.dot(merged, wo_ref[...], preferred_element_type=F32)


def _mixer(z, x, b_gate, pool_scale, w_pool_bf, ln_g, ln_b, w_s, b_s_t, w_gproj_bf, w_out_bf, tm=256):
    const = lambda *shape: pl.BlockSpec(shape, lambda i: (0,) * len(shape), pipeline_mode=pl.Buffered(1))
    return pl.pallas_call(
        functools.partial(_mixer_kernel, tm=tm),
        grid=(SEQ // tm,),
        in_specs=[
            pl.BlockSpec((tm, IN_WIDTH), lambda i: (i, 0)),
            pl.BlockSpec((tm, D_MODEL), lambda i: (i, 0)),
            const(1, 2 * D_MODEL),
            const(1, POOL_WIDTH),
            const(len(POOL_WINDOWS), POOL_GROUP_WIDTH, POOL_OUT_GROUP),
            const(1, GMLP_WIDTH),
            const(1, GMLP_WIDTH),
            const(GMLP_HEADS, CHUNK, CHUNK),
            const(CHUNK, GMLP_HEADS),
            const(GMLP_WIDTH, D_MODEL),
            const(D_MODEL, D_MODEL),
        ],
        out_specs=pl.BlockSpec((tm, D_MODEL), lambda i: (i, 0)),
        out_shape=jax.ShapeDtypeStruct((SEQ, D_MODEL), F32),
        scratch_shapes=[
            pltpu.VMEM((MAX_WINDOW, POOL_WIDTH), F32),
            pltpu.VMEM((MAX_WINDOW + tm, POOL_WIDTH), F32),
            pltpu.VMEM((tm, GMLP_WIDTH), BF16),
            pltpu.VMEM((tm, D_MODEL), F32),
        ],
        compiler_params=pltpu.CompilerParams(
            dimension_semantics=("arbitrary",), vmem_limit_bytes=VMEM_LIMIT),
        name="mixer",
    )(z, x, b_gate, pool_scale, w_pool_bf, ln_g, ln_b, w_s, b_s_t, w_gproj_bf, w_out_bf)


N_LIST = TOPK + 1
LIST_ROWS = 24


def _top_values(s, n):
    rows = []
    w = s
    for _ in range(n):
        m = jnp.max(w, axis=0, keepdims=True)
        rows.append(m)
        w = jnp.where(w == m, NEG, w)
    return rows


def _stack_rows(rows, n_rows):
    iota = lax.broadcasted_iota(jnp.int32, (n_rows, LANES), 0)
    out = jnp.full((n_rows, LANES), NEG, F32)
    for k, r in enumerate(rows):
        out = jnp.where(iota == k, r, out)
    return out


def _route_chunk(s1, s2):
    r1 = _top_values(s1, N_LIST)
    r2 = _top_values(s2, N_LIST)
    t1 = _stack_rows(r1, LIST_ROWS)
    t2 = _stack_rows(r2, LIST_ROWS)
    slabs = [t2 + r1[0]]
    for a in range(1, 8):
        slabs.append(t2[0:8] + r1[a])
    slabs.append(t1[8:LIST_ROWS] + r2[0])
    cand = jnp.concatenate(slabs, axis=0)
    sums = _top_values(cand, N_LIST)
    z = jnp.zeros_like(sums[0])
    for k in range(TOPK):
        z = z + jnp.exp(sums[k] - sums[0])
    tau = 0.5 * (sums[TOPK - 1] + sums[TOPK])
    thr = tau - s1
    e1 = jnp.exp(s1 - r1[0])
    e2 = jnp.exp(s2 - r2[0]) / z
    return thr, e1, e2


def _route_kernel(x_ref, g_ref, wq_ref, k_ref, h2t_ref, thr_ref, e1_ref, s2_ref, e2_ref, h2_ref, *, tc):
    @pl.when(pl.program_id(1) == 0)
    def _():
        h2 = _rms(x_ref[...], g_ref[...])
        h2_ref[...] = h2.astype(BF16)
        h2t_ref[...] = h2.T.astype(BF16)

    q = jnp.dot(h2_ref[...], wq_ref[...], preferred_element_type=F32)
    k1 = k_ref[0, 0]
    k2 = k_ref[0, 1]
    dims = (((1,), (1,)), ((), ()))
    for c in range(tc // LANES):
        qc = q[c * LANES:(c + 1) * LANES]
        s1 = lax.dot_general(k1, qc[:, 0:HALF_DIM], dims, precision=lax.Precision.HIGHEST,
                             preferred_element_type=F32)
        s2 = lax.dot_general(k2, qc[:, HALF_DIM:QUERY_DIM], dims, precision=lax.Precision.HIGHEST,
                             preferred_element_type=F32)
        thr, e1, e2 = _route_chunk(s1, s2)
        thr_ref[0, c] = thr
        e1_ref[0, c] = e1
        s2_ref[0, c] = s2
        e2_ref[0, c] = e2


def _route(x1, g, w_q_bf, sub_keys, tc=512):
    n_chunks = SEQ // LANES
    r_shape = jax.ShapeDtypeStruct((PEER_HEADS, n_chunks, N_KEYS, LANES), F32)
    r_spec = pl.BlockSpec((1, tc // LANES, N_KEYS, LANES), lambda i, h: (h, i, 0, 0))
    return pl.pallas_call(
        functools.partial(_route_kernel, tc=tc),
        grid=(SEQ // tc, PEER_HEADS),
        in_specs=[
            pl.BlockSpec((tc, D_MODEL), lambda i, h: (i, 0)),
            pl.BlockSpec((1, D_MODEL), lambda i, h: (0, 0)),
            pl.BlockSpec((D_MODEL, QUERY_DIM), lambda i, h: (0, h)),
            pl.BlockSpec((1, 2, N_KEYS, HALF_DIM), lambda i, h: (h, 0, 0, 0)),
        ],
        out_specs=[pl.BlockSpec((D_MODEL, tc), lambda i, h: (0, i)), r_spec, r_spec, r_spec, r_spec],
        out_shape=[jax.ShapeDtypeStruct((D_MODEL, SEQ), BF16), r_shape, r_shape, r_shape, r_shape],
        scratch_shapes=[pltpu.VMEM((tc, D_MODEL), BF16)],
        compiler_params=pltpu.CompilerParams(
            dimension_semantics=("arbitrary", "arbitrary"), vmem_limit_bytes=VMEM_LIMIT),
        name="route",
    )(x1, g, w_q_bf, sub_keys)


def _experts_kernel(u_ref, vt_ref, xt_ref, thr_ref, e1_ref, s2_ref, e2_ref, y_ref, act_ref, p_ref, *, tt, eb):
    @pl.when(pl.program_id(1) == 0)
    def _():
        y_ref[...] = jnp.zeros_like(y_ref)

    act_ref[...] = jnp.dot(u_ref[...], xt_ref[...], preferred_element_type=F32)
    for ib in range(eb // N_KEYS):
        rs = slice(ib * N_KEYS, (ib + 1) * N_KEYS)
        for c in range(tt // LANES):
            cs = slice(c * LANES, (c + 1) * LANES)
            gate = jnp.zeros((N_KEYS, LANES), F32)
            for h in range(PEER_HEADS):
                thr = thr_ref[h, c, ib:ib + 1, :]
                e1 = e1_ref[h, c, ib:ib + 1, :]
                gate = gate + jnp.where(s2_ref[h, c] >= thr, e2_ref[h, c], 0.0) * e1
            p_ref[rs, cs] = (gate * jax.nn.gelu(act_ref[rs, cs])).astype(BF16)
    y_ref[...] += jnp.dot(vt_ref[...], p_ref[...], preferred_element_type=F32)


def _experts(u_bf, vt_bf, h2t, thr, e1, s2, e2, tt=512, eb=1024):
    ib = eb // N_KEYS
    nc = tt // LANES
    row_spec = pl.BlockSpec((PEER_HEADS, nc, ib, LANES), lambda t, e: (0, t, e, 0))
    full_spec = pl.BlockSpec((PEER_HEADS, nc, N_KEYS, LANES), lambda t, e: (0, t, 0, 0))
    return pl.pallas_call(
        functools.partial(_experts_kernel, tt=tt, eb=eb),
        grid=(SEQ // tt, N_EXPERTS // eb),
        in_specs=[
            pl.BlockSpec((eb, D_MODEL), lambda t, e: (e, 0)),
            pl.BlockSpec((D_MODEL, eb), lambda t, e: (0, e)),
            pl.BlockSpec((D_MODEL, tt), lambda t, e: (0, t)),
            row_spec, row_spec, full_spec, full_spec,
        ],
        out_specs=pl.BlockSpec((D_MODEL, tt), lambda t, e: (0, t)),
        out_shape=jax.ShapeDtypeStruct((D_MODEL, SEQ), F32),
        scratch_shapes=[pltpu.VMEM((eb, tt), F32), pltpu.VMEM((eb, tt), BF16)],
        compiler_params=pltpu.CompilerParams(
            dimension_semantics=("arbitrary", "arbitrary"), vmem_limit_bytes=VMEM_LIMIT),
        name="experts",
    )(u_bf, vt_bf, h2t, thr, e1, s2, e2)


def _final_kernel(x_ref, yt_ref, g_ref, o_ref):
    o_ref[...] = _rms(x_ref[...] + yt_ref[...].T, g_ref[...])


def _final(x1, y_t, g, tm=512):
    return pl.pallas_call(
        _final_kernel,
        grid=(SEQ // tm,),
        in_specs=[
            pl.BlockSpec((tm, D_MODEL), lambda i: (i, 0)),
            pl.BlockSpec((D_MODEL, tm), lambda i: (0, i)),
            pl.BlockSpec((1, D_MODEL), lambda i: (0, 0)),
        ],
        out_specs=pl.BlockSpec((tm, D_MODEL), lambda i: (i, 0)),
        out_shape=jax.ShapeDtypeStruct((SEQ, D_MODEL), F32),
        compiler_params=pltpu.CompilerParams(
            dimension_semantics=("arbitrary",), vmem_limit_bytes=VMEM_LIMIT),
        name="final",
    )(x1, y_t, g)


def kernel(x, norm1_g, w_in, b_gate, pool_scale, w_pool, ln_v_g, ln_v_b, w_s, b_s, w_gproj, w_out,
           norm2_g, w_q, sub_keys, expert_u, expert_v, final_g):
    assert x.shape == (1, SEQ, D_MODEL)
    x2 = x.reshape(SEQ, D_MODEL)
    row = lambda v: v.reshape(1, -1)

    z = _in_proj(x2, row(norm1_g), w_in.astype(BF16))
    x1 = _mixer(z, x2, row(b_gate), row(pool_scale), w_pool.astype(BF16), row(ln_v_g), row(ln_v_b),
                w_s, b_s.T, w_gproj.astype(BF16), w_out.astype(BF16))
    h2t, thr, e1, s2, e2 = _route(x1, row(norm2_g), w_q.astype(BF16), sub_keys)
    y_t = _experts(expert_u.astype(BF16), expert_v.T.astype(BF16), h2t, thr, e1, s2, e2)
    out = _final(x1, y_t, row(final_g))
    return out.reshape(1, SEQ, D_MODEL)
```

```python
import functools

import jax
import jax.numpy as jnp
from jax import lax
from jax.experimental import pallas as pl
from jax.experimental.pallas import tpu as pltpu

D_MODEL = 2048
SEQ = 8192
POOL_WINDOWS = (2, 4, 8, 16)
POOL_WIDTH = 1024
POOL_GROUP_WIDTH = 256
POOL_OUT_GROUP = 512
GMLP_HEADS = 8
GMLP_HEAD_DIM = 128
GMLP_WIDTH = 1024
CHUNK = 128
IN_WIDTH = POOL_WIDTH + 2 * GMLP_WIDTH + 2 * D_MODEL
PEER_HEADS = 8
N_KEYS = 128
N_EXPERTS = N_KEYS * N_KEYS
HALF_DIM = 128
QUERY_DIM = 256
TOPK = 16
EPS = 1e-6

LANES = 128
MXU_DIM = 256
MAX_WINDOW = max(POOL_WINDOWS)
NEG = -1e30
VMEM_LIMIT = 56 * 1024 * 1024

F32 = jnp.float32
BF16 = jnp.bfloat16


def _rms(x, g):
    return x * lax.rsqrt(jnp.mean(x * x, axis=-1, keepdims=True) + EPS) * g


def _in_proj_kernel(x_ref, g_ref, w_ref, z_ref, h_ref):
    @pl.when(pl.program_id(1) == 0)
    def _():
        h_ref[...] = _rms(x_ref[...], g_ref[...]).astype(BF16)

    z_ref[...] = jnp.dot(h_ref[...], w_ref[...], preferred_element_type=F32)


def _in_proj(x, g, w_in_bf, tm=1024, tn=1024):
    return pl.pallas_call(
        _in_proj_kernel,
        grid=(SEQ // tm, IN_WIDTH // tn),
        in_specs=[
            pl.BlockSpec((tm, D_MODEL), lambda i, j: (i, 0)),
            pl.BlockSpec((1, D_MODEL), lambda i, j: (0, 0)),
            pl.BlockSpec((D_MODEL, tn), lambda i, j: (0, j)),
        ],
        out_specs=pl.BlockSpec((tm, tn), lambda i, j: (i, j)),
        out_shape=jax.ShapeDtypeStruct((SEQ, IN_WIDTH), F32),
        scratch_shapes=[pltpu.VMEM((tm, D_MODEL), BF16)],
        compiler_params=pltpu.CompilerParams(
            dimension_semantics=("arbitrary", "arbitrary"), vmem_limit_bytes=VMEM_LIMIT),
        name="in_proj",
    )(x, g, w_in_bf)


def _mixer_kernel(z_ref, x_ref, bg_ref, ps_ref, wp_ref, lng_ref, lnb_ref, ws_ref, bst_ref,
                  wg_ref, wo_ref, o_ref, halo_ref, ext_ref, ysgu_ref, ypool_ref, *, tm):
    i = pl.program_id(0)

    @pl.when(i == 0)
    def _():
        halo_ref[...] = jnp.zeros_like(halo_ref)

    a = z_ref[:, 0:POOL_WIDTH]
    ext_ref[0:MAX_WINDOW, :] = halo_ref[...]
    ext_ref[MAX_WINDOW:MAX_WINDOW + tm, :] = a
    halo_ref[...] = a[tm - MAX_WINDOW:tm, :]
    pos1 = (i * tm + 1 + lax.broadcasted_iota(jnp.int32, (tm, POOL_GROUP_WIDTH), 0)).astype(F32)
    for gi, w in enumerate(POOL_WINDOWS):
        lo = gi * POOL_GROUP_WIDTH
        hi = lo + POOL_GROUP_WIDTH
        win = ext_ref[MAX_WINDOW:MAX_WINDOW + tm, lo:hi]
        for k in range(1, w):
            win = win + ext_ref[MAX_WINDOW - k:MAX_WINDOW - k + tm, lo:hi]
        cnt = jnp.minimum(pos1, float(w))
        pooled = (win / cnt - ext_ref[MAX_WINDOW:MAX_WINDOW + tm, lo:hi]) * ps_ref[:, lo:hi]
        ypool_ref[:, gi * POOL_OUT_GROUP:(gi + 1) * POOL_OUT_GROUP] = jnp.dot(
            pooled.astype(BF16), wp_ref[gi], preferred_element_type=F32)

    u = jax.nn.gelu(z_ref[:, POOL_WIDTH:POOL_WIDTH + GMLP_WIDTH])
    vg = jax.nn.gelu(z_ref[:, POOL_WIDTH + GMLP_WIDTH:POOL_WIDTH + 2 * GMLP_WIDTH])
    mu = jnp.mean(vg, axis=-1, keepdims=True)
    vc = vg - mu
    var = jnp.mean(vc * vc, axis=-1, keepdims=True)
    vn = (vc * lax.rsqrt(var + EPS) * lng_ref[...] + lnb_ref[...]).astype(BF16)
    row = lax.broadcasted_iota(jnp.int32, (CHUNK, CHUNK), 0)
    col = lax.broadcasted_iota(jnp.int32, (CHUNK, CHUNK), 1)
    for h in range(GMLP_HEADS):
        wsm = jnp.where(row >= col, ws_ref[h], 0.0).astype(BF16)
        bias = bst_ref[:, h:h + 1]
        cs = slice(h * GMLP_HEAD_DIM, (h + 1) * GMLP_HEAD_DIM)
        for c in range(tm // CHUNK):
            rs = slice(c * CHUNK, (c + 1) * CHUNK)
            sv = jnp.dot(wsm, vn[rs, cs], preferred_element_type=F32) + bias
            ysgu_ref[rs, cs] = (u[rs, cs] * sv).astype(BF16)
    y_sgu = jnp.dot(ysgu_ref[...], wg_ref[...], preferred_element_type=F32)

    g_off = POOL_WIDTH + 2 * GMLP_WIDTH
    ga = jax.nn.sigmoid(z_ref[:, g_off:g_off + D_MODEL] + bg_ref[:, 0:D_MODEL])
    gb = jax.nn.sigmoid(z_ref[:, g_off + D_MODEL:g_off + 2 * D_MODEL] + bg_ref[:, D_MODEL:2 * D_MODEL])
    merged = (ga * ypool_ref[...] + gb * y_sgu).astype(BF16)
    o_ref[...] = x_ref[...] + jnp.dot(merged, wo_ref[...], preferred_element_type=F32)


def _mixer(z, x, b_gate, pool_scale, w_pool_bf, ln_g, ln_b, w_s, b_s_t, w_gproj_bf, w_out_bf, tm=256):
    const = lambda *shape: pl.BlockSpec(shape, lambda i: (0,) * len(shape), pipeline_mode=pl.Buffered(1))
    return pl.pallas_call(
        functools.partial(_mixer_kernel, tm=tm),
        grid=(SEQ // tm,),
        in_specs=[
            pl.BlockSpec((tm, IN_WIDTH), lambda i: (i, 0)),
            pl.BlockSpec((tm, D_MODEL), lambda i: (i, 0)),
            const(1, 2 * D_MODEL),
            const(1, POOL_WIDTH),
            const(len(POOL_WINDOWS), POOL_GROUP_WIDTH, POOL_OUT_GROUP),
            const(1, GMLP_WIDTH),
            const(1, GMLP_WIDTH),
            const(GMLP_HEADS, CHUNK, CHUNK),
            const(CHUNK, GMLP_HEADS),
            const(GMLP_WIDTH, D_MODEL),
            const(D_MODEL, D_MODEL),
        ],
        out_specs=pl.BlockSpec((tm, D_MODEL), lambda i: (i, 0)),
        out_shape=jax.ShapeDtypeStruct((SEQ, D_MODEL), F32),
        scratch_shapes=[
            pltpu.VMEM((MAX_WINDOW, POOL_WIDTH), F32),
            pltpu.VMEM((MAX_WINDOW + tm, POOL_WIDTH), F32),
            pltpu.VMEM((tm, GMLP_WIDTH), BF16),
            pltpu.VMEM((tm, D_MODEL), F32),
        ],
        compiler_params=pltpu.CompilerParams(
            dimension_semantics=("arbitrary",), vmem_limit_bytes=VMEM_LIMIT),
        name="mixer",
    )(z, x, b_gate, pool_scale, w_pool_bf, ln_g, ln_b, w_s, b_s_t, w_gproj_bf, w_out_bf)


N_LIST = TOPK + 1
LIST_ROWS = 24


def _col_max(w3):
    m = jnp.max(w3, axis=0)
    for shift in (4, 2, 1):
        m = jnp.maximum(m, pltpu.roll(m, shift, axis=0))
    return m


def _top_values(w3, n):
    rows = []
    for _ in range(n):
        m = _col_max(w3)
        rows.append(m)
        w3 = jnp.where(w3 == m[None], NEG, w3)
    return rows


def _stack_rows(rows, n_groups):
    sub = lax.broadcasted_iota(jnp.int32, (8, LANES), 0)
    groups = []
    for gidx in range(n_groups):
        out = jnp.full((8, LANES), NEG, F32)
        for k in range(gidx * 8, min(len(rows), gidx * 8 + 8)):
            out = jnp.where(sub == k - gidx * 8, rows[k], out)
        groups.append(out)
    return groups


def _route_chunk(s1, s2):
    s1v = s1.reshape(N_KEYS // 8, 8, LANES)
    s2v = s2.reshape(N_KEYS // 8, 8, LANES)
    r1 = _top_values(s1v, N_LIST)
    r2 = _top_values(s2v, N_LIST)
    t1 = _stack_rows(r1, LIST_ROWS // 8)
    t2 = _stack_rows(r2, LIST_ROWS // 8)
    slabs = [g + r1[0] for g in t2]
    slabs += [t2[0] + r1[a] for a in range(1, 8)]
    slabs += [g + r2[0] for g in t1[1:]]
    sums = _top_values(jnp.stack(slabs, axis=0), N_LIST)
    z = jnp.zeros_like(sums[0])
    for k in range(TOPK):
        z = z + jnp.exp(sums[k] - sums[0])
    tau = 0.5 * (sums[TOPK - 1] + sums[TOPK])
    scale = (0.5 / z)[None]
    e1 = jnp.exp(s1v - r1[0][None])
    e2 = jnp.exp(s2v - r2[0][None]) * scale
    th = jnp.exp((tau - r2[0])[None] - s1v) * scale
    shape = (N_KEYS, LANES)
    return th.reshape(shape), e1.reshape(shape), e2.reshape(shape)


def _route_kernel(x_ref, g_ref, wq_ref, k_ref, h2t_ref, th_ref, e1_ref, e2_ref, h2_ref, *, tc):
    @pl.when(pl.program_id(1) == 0)
    def _():
        h2 = _rms(x_ref[...], g_ref[...])
        h2_ref[...] = h2.astype(BF16)
        h2t_ref[...] = h2.T.astype(BF16)

    q = jnp.dot(h2_ref[...], wq_ref[...], preferred_element_type=F32)
    k1 = k_ref[0, 0]
    k2 = k_ref[0, 1]
    dims = (((1,), (1,)), ((), ()))
    for c in range(tc // LANES):
        qc = q[c * LANES:(c + 1) * LANES]
        s1 = lax.dot_general(k1, qc[:, 0:HALF_DIM], dims, precision=lax.Precision.HIGHEST,
                             preferred_element_type=F32)
        s2 = lax.dot_general(k2, qc[:, HALF_DIM:QUERY_DIM], dims, precision=lax.Precision.HIGHEST,
                             preferred_element_type=F32)
        th, e1, e2 = _route_chunk(s1, s2)
        th_ref[0, c] = th
        e1_ref[0, c] = e1
        e2_ref[0, c] = e2


def _route(x1, g, w_q_bf, sub_keys, tc=512):
    n_chunks = SEQ // LANES
    r_shape = jax.ShapeDtypeStruct((PEER_HEADS, n_chunks, N_KEYS, LANES), F32)
    r_spec = pl.BlockSpec((1, tc // LANES, N_KEYS, LANES), lambda i, h: (h, i, 0, 0))
    return pl.pallas_call(
        functools.partial(_route_kernel, tc=tc),
        grid=(SEQ // tc, PEER_HEADS),
        in_specs=[
            pl.BlockSpec((tc, D_MODEL), lambda i, h: (i, 0)),
            pl.BlockSpec((1, D_MODEL), lambda i, h: (0, 0)),
            pl.BlockSpec((D_MODEL, QUERY_DIM), lambda i, h: (0, h)),
            pl.BlockSpec((1, 2, N_KEYS, HALF_DIM), lambda i, h: (h, 0, 0, 0)),
        ],
        out_specs=[pl.BlockSpec((D_MODEL, tc), lambda i, h: (0, i)), r_spec, r_spec, r_spec],
        out_shape=[jax.ShapeDtypeStruct((D_MODEL, SEQ), BF16), r_shape, r_shape, r_shape],
        scratch_shapes=[pltpu.VMEM((tc, D_MODEL), BF16)],
        compiler_params=pltpu.CompilerParams(
            dimension_semantics=("arbitrary", "arbitrary"), vmem_limit_bytes=VMEM_LIMIT),
        name="route",
    )(x1, g, w_q_bf, sub_keys)


GELU_C = 0.7978845608028654
GELU_K = 0.044715


def _gated_act_tile(a, gate):
    t = (a * a) * (GELU_C * GELU_K) + GELU_C
    return gate * (a + a * jnp.tanh(a * t))


def _experts_step(u_ref, vt_ref, xt_ref, th_ref, e1_ref, e2_ref, y_ref,
                  act_w, act_r, p_w, p_r, *, tt, eb):
    def gate_tile(ib, c):
        rs = slice(ib * N_KEYS, (ib + 1) * N_KEYS)
        cs = slice(c * LANES, (c + 1) * LANES)
        gate = jnp.zeros((N_KEYS, LANES), F32)
        for h in range(PEER_HEADS):
            th = th_ref[h, c, ib:ib + 1, :]
            e1 = e1_ref[h, c, ib:ib + 1, :]
            e2 = e2_ref[h, c]
            gate = gate + jnp.where(e2 >= th, e2, 0.0) * e1
        p_w[rs, cs] = _gated_act_tile(act_r[rs, cs], gate).astype(BF16)

    n_rb = eb // MXU_DIM
    n_nb = tt // MXU_DIM
    d_rows = D_MODEL // n_rb
    for rb in range(n_rb):
        rows = slice(rb * MXU_DIM, (rb + 1) * MXU_DIM)
        drows = slice(rb * d_rows, (rb + 1) * d_rows)
        for nb in range(n_nb):
            cols = slice(nb * MXU_DIM, (nb + 1) * MXU_DIM)
            act_w[rows, cols] = jnp.dot(u_ref[rows, :], xt_ref[:, cols], preferred_element_type=F32)
            for ib in range(rb * MXU_DIM // N_KEYS, (rb + 1) * MXU_DIM // N_KEYS):
                for c in range(nb * MXU_DIM // LANES, (nb + 1) * MXU_DIM // LANES):
                    gate_tile(ib, c)
            y_ref[drows, cols] += jnp.dot(vt_ref[drows, :], p_r[:, cols], preferred_element_type=F32)


def _experts_kernel(u_ref, vt_ref, xt_ref, th_ref, e1_ref, e2_ref, y_ref,
                    act0, act1, p0, p1, *, tt, eb, n_eb, n_steps):
    g = pl.program_id(0)
    acc_step = jnp.clip(g - 2, 0, n_steps - 1)

    @pl.when(g == 0)
    def _():
        act1[...] = jnp.zeros_like(act1)
        p1[...] = jnp.zeros_like(p1)

    @pl.when(acc_step % n_eb == 0)
    def _():
        y_ref[...] = jnp.zeros_like(y_ref)

    step = functools.partial(_experts_step, u_ref, vt_ref, xt_ref, th_ref, e1_ref, e2_ref, y_ref,
                             tt=tt, eb=eb)

    @pl.when(g % 2 == 0)
    def _():
        step(act0, act1, p0, p1)

    @pl.when(g % 2 == 1)
    def _():
        step(act1, act0, p1, p0)


def _experts(u_bf, vt_bf, h2t, th, e1, e2, tt=512, eb=1024):
    ib = eb // N_KEYS
    nc = tt // LANES
    n_eb = N_EXPERTS // eb
    n_steps = (SEQ // tt) * n_eb
    s_act = lambda g: jnp.minimum(g, n_steps - 1)
    s_gate = lambda g: jnp.clip(g - 1, 0, n_steps - 1)
    s_acc = lambda g: jnp.clip(g - 2, 0, n_steps - 1)
    row_spec = pl.BlockSpec((PEER_HEADS, nc, ib, LANES), lambda g: (0, s_gate(g) // n_eb, s_gate(g) % n_eb, 0))
    full_spec = pl.BlockSpec((PEER_HEADS, nc, N_KEYS, LANES), lambda g: (0, s_gate(g) // n_eb, 0, 0))
    return pl.pallas_call(
        functools.partial(_experts_kernel, tt=tt, eb=eb, n_eb=n_eb, n_steps=n_steps),
        grid=(n_steps + 2,),
        in_specs=[
            pl.BlockSpec((eb, D_MODEL), lambda g: (s_act(g) % n_eb, 0)),
            pl.BlockSpec((D_MODEL, eb), lambda g: (0, s_acc(g) % n_eb)),
            pl.BlockSpec((D_MODEL, tt), lambda g: (0, s_act(g) // n_eb)),
            row_spec, row_spec, full_spec,
        ],
        out_specs=pl.BlockSpec((D_MODEL, tt), lambda g: (0, s_acc(g) // n_eb)),
        out_shape=jax.ShapeDtypeStruct((D_MODEL, SEQ), F32),
        scratch_shapes=[pltpu.VMEM((eb, tt), F32), pltpu.VMEM((eb, tt), F32),
                        pltpu.VMEM((eb, tt), BF16), pltpu.VMEM((eb, tt), BF16)],
        compiler_params=pltpu.CompilerParams(
            dimension_semantics=("arbitrary",), vmem_limit_bytes=VMEM_LIMIT),
        name="experts",
    )(u_bf, vt_bf, h2t, th, e1, e2)


def _final_kernel(x_ref, yt_ref, g_ref, o_ref):
    o_ref[...] = _rms(x_ref[...] + yt_ref[...].T, g_ref[...])


def _final(x1, y_t, g, tm=512):
    return pl.pallas_call(
        _final_kernel,
        grid=(SEQ // tm,),
        in_specs=[
            pl.BlockSpec((tm, D_MODEL), lambda i: (i, 0)),
            pl.BlockSpec((D_MODEL, tm), lambda i: (0, i)),
            pl.BlockSpec((1, D_MODEL), lambda i: (0, 0)),
        ],
        out_specs=pl.BlockSpec((tm, D_MODEL), lambda i: (i, 0)),
        out_shape=jax.ShapeDtypeStruct((SEQ, D_MODEL), F32),
        compiler_params=pltpu.CompilerParams(
            dimension_semantics=("arbitrary",), vmem_limit_bytes=VMEM_LIMIT),
        name="final",
    )(x1, y_t, g)


def kernel(x, norm1_g, w_in, b_gate, pool_scale, w_pool, ln_v_g, ln_v_b, w_s, b_s, w_gproj, w_out,
           norm2_g, w_q, sub_keys, expert_u, expert_v, final_g):
    assert x.shape == (1, SEQ, D_MODEL)
    x2 = x.reshape(SEQ, D_MODEL)
    row = lambda v: v.reshape(1, -1)

    z = _in_proj(x2, row(norm1_g), w_in.astype(BF16))
    x1 = _mixer(z, x2, row(b_gate), row(pool_scale), w_pool.astype(BF16), row(ln_v_g), row(ln_v_b),
                w_s, b_s.T, w_gproj.astype(BF16), w_out.astype(BF16))
    h2t, th, e1, e2 = _route(x1, row(norm2_g), w_q.astype(BF16), sub_keys)
    y_t = _experts(expert_u.astype(BF16), expert_v.T.astype(BF16), h2t, th, e1, e2)
    out = _final(x1, y_t, row(final_g))
    return out.reshape(1, SEQ, D_MODEL)
```

```python
import functools

import jax
import jax.numpy as jnp
from jax import lax
from jax.experimental import pallas as pl
from jax.experimental.pallas import tpu as pltpu

D_MODEL = 2048
SEQ = 8192
POOL_WINDOWS = (2, 4, 8, 16)
POOL_WIDTH = 1024
POOL_GROUP_WIDTH = 256
POOL_OUT_GROUP = 512
GMLP_HEADS = 8
GMLP_HEAD_DIM = 128
GMLP_WIDTH = 1024
CHUNK = 128
IN_WIDTH = POOL_WIDTH + 2 * GMLP_WIDTH + 2 * D_MODEL
PEER_HEADS = 8
N_KEYS = 128
N_EXPERTS = N_KEYS * N_KEYS
HALF_DIM = 128
QUERY_DIM = 256
TOPK = 16
EPS = 1e-6

LANES = 128
MXU_DIM = 256
TOKEN_TILE = 512
EXPERT_BLOCK = 1024
MAX_WINDOW = max(POOL_WINDOWS)
NEG = -1e30
VMEM_LIMIT = 56 * 1024 * 1024

F32 = jnp.float32
BF16 = jnp.bfloat16


def _rms(x, g):
    return x * lax.rsqrt(jnp.mean(x * x, axis=-1, keepdims=True) + EPS) * g


def _in_proj_kernel(x_ref, g_ref, w_ref, z_ref, h_ref):
    @pl.when(pl.program_id(1) == 0)
    def _():
        h_ref[...] = _rms(x_ref[...], g_ref[...]).astype(BF16)

    z_ref[...] = jnp.dot(h_ref[...], w_ref[...], preferred_element_type=F32)


def _in_proj(x, g, w_in_bf, tm=1024, tn=1024):
    return pl.pallas_call(
        _in_proj_kernel,
        grid=(SEQ // tm, IN_WIDTH // tn),
        in_specs=[
            pl.BlockSpec((tm, D_MODEL), lambda i, j: (i, 0)),
            pl.BlockSpec((1, D_MODEL), lambda i, j: (0, 0)),
            pl.BlockSpec((D_MODEL, tn), lambda i, j: (0, j)),
        ],
        out_specs=pl.BlockSpec((tm, tn), lambda i, j: (i, j)),
        out_shape=jax.ShapeDtypeStruct((SEQ, IN_WIDTH), F32),
        scratch_shapes=[pltpu.VMEM((tm, D_MODEL), BF16)],
        compiler_params=pltpu.CompilerParams(
            dimension_semantics=("arbitrary", "arbitrary"), vmem_limit_bytes=VMEM_LIMIT),
        name="in_proj",
    )(x, g, w_in_bf)


def _mixer_kernel(z_ref, x_ref, bg_ref, ps_ref, wp_ref, lng_ref, lnb_ref, ws_ref, bst_ref,
                  wg_ref, wo_ref, o_ref, halo_ref, ext_ref, ysgu_ref, ypool_ref, *, tm):
    i = pl.program_id(0)

    @pl.when(i == 0)
    def _():
        halo_ref[...] = jnp.zeros_like(halo_ref)

    a = z_ref[:, 0:POOL_WIDTH]
    ext_ref[0:MAX_WINDOW, :] = halo_ref[...]
    ext_ref[MAX_WINDOW:MAX_WINDOW + tm, :] = a
    halo_ref[...] = a[tm - MAX_WINDOW:tm, :]
    pos1 = (i * tm + 1 + lax.broadcasted_iota(jnp.int32, (tm, POOL_GROUP_WIDTH), 0)).astype(F32)
    for gi, w in enumerate(POOL_WINDOWS):
        lo = gi * POOL_GROUP_WIDTH
        hi = lo + POOL_GROUP_WIDTH
        win = ext_ref[MAX_WINDOW:MAX_WINDOW + tm, lo:hi]
        for k in range(1, w):
            win = win + ext_ref[MAX_WINDOW - k:MAX_WINDOW - k + tm, lo:hi]
        cnt = jnp.minimum(pos1, float(w))
        pooled = (win / cnt - ext_ref[MAX_WINDOW:MAX_WINDOW + tm, lo:hi]) * ps_ref[:, lo:hi]
        ypool_ref[:, gi * POOL_OUT_GROUP:(gi + 1) * POOL_OUT_GROUP] = jnp.dot(
            pooled.astype(BF16), wp_ref[gi], preferred_element_type=F32)

    u = jax.nn.gelu(z_ref[:, POOL_WIDTH:POOL_WIDTH + GMLP_WIDTH])
    vg = jax.nn.gelu(z_ref[:, POOL_WIDTH + GMLP_WIDTH:POOL_WIDTH + 2 * GMLP_WIDTH])
    mu = jnp.mean(vg, axis=-1, keepdims=True)
    vc = vg - mu
    var = jnp.mean(vc * vc, axis=-1, keepdims=True)
    vn = (vc * lax.rsqrt(var + EPS) * lng_ref[...] + lnb_ref[...]).astype(BF16)
    row = lax.broadcasted_iota(jnp.int32, (CHUNK, CHUNK), 0)
    col = lax.broadcasted_iota(jnp.int32, (CHUNK, CHUNK), 1)
    for h in range(GMLP_HEADS):
        wsm = jnp.where(row >= col, ws_ref[h], 0.0).astype(BF16)
        bias = bst_ref[:, h:h + 1]
        cs = slice(h * GMLP_HEAD_DIM, (h + 1) * GMLP_HEAD_DIM)
        for c in range(tm // CHUNK):
            rs = slice(c * CHUNK, (c + 1) * CHUNK)
            sv = jnp.dot(wsm, vn[rs, cs], preferred_element_type=F32) + bias
            ysgu_ref[rs, cs] = (u[rs, cs] * sv).astype(BF16)
    y_sgu = jnp.dot(ysgu_ref[...], wg_ref[...], preferred_element_type=F32)

    g_off = POOL_WIDTH + 2 * GMLP_WIDTH
    ga = jax.nn.sigmoid(z_ref[:, g_off:g_off + D_MODEL] + bg_ref[:, 0:D_MODEL])
    gb = jax.nn.sigmoid(z_ref[:, g_off + D_MODEL:g_off + 2 * D_MODEL] + bg_ref[:, D_MODEL:2 * D_MODEL])
    merged = (ga * ypool_ref[...] + gb * y_sgu).astype(BF16)
    o_ref[...] = x_ref[...] + jnp.dot(merged, wo_ref[...], preferred_element_type=F32)


def _mixer(z, x, b_gate, pool_scale, w_pool_bf, ln_g, ln_b, w_s, b_s_t, w_gproj_bf, w_out_bf, tm=256):
    const = lambda *shape: pl.BlockSpec(shape, lambda i: (0,) * len(shape), pipeline_mode=pl.Buffered(1))
    return pl.pallas_call(
        functools.partial(_mixer_kernel, tm=tm),
        grid=(SEQ // tm,),
        in_specs=[
            pl.BlockSpec((tm, IN_WIDTH), lambda i: (i, 0)),
            pl.BlockSpec((tm, D_MODEL), lambda i: (i, 0)),
            const(1, 2 * D_MODEL),
            const(1, POOL_WIDTH),
            const(len(POOL_WINDOWS), POOL_GROUP_WIDTH, POOL_OUT_GROUP),
            const(1, GMLP_WIDTH),
            const(1, GMLP_WIDTH),
            const(GMLP_HEADS, CHUNK, CHUNK),
            const(CHUNK, GMLP_HEADS),
            const(GMLP_WIDTH, D_MODEL),
            const(D_MODEL, D_MODEL),
        ],
        out_specs=pl.BlockSpec((tm, D_MODEL), lambda i: (i, 0)),
        out_shape=jax.ShapeDtypeStruct((SEQ, D_MODEL), F32),
        scratch_shapes=[
            pltpu.VMEM((MAX_WINDOW, POOL_WIDTH), F32),
            pltpu.VMEM((MAX_WINDOW + tm, POOL_WIDTH), F32),
            pltpu.VMEM((tm, GMLP_WIDTH), BF16),
            pltpu.VMEM((tm, D_MODEL), F32),
        ],
        compiler_params=pltpu.CompilerParams(
            dimension_semantics=("arbitrary",), vmem_limit_bytes=VMEM_LIMIT),
        name="mixer",
    )(z, x, b_gate, pool_scale, w_pool_bf, ln_g, ln_b, w_s, b_s_t, w_gproj_bf, w_out_bf)


N_LIST = TOPK + 1
LIST_ROWS = 24


def _col_max(w3):
    m = jnp.max(w3, axis=0)
    for shift in (4, 2, 1):
        m = jnp.maximum(m, pltpu.roll(m, shift, axis=0))
    return m


def _top_values(w3, n):
    rows = []
    for _ in range(n):
        m = _col_max(w3)
        rows.append(m)
        w3 = jnp.where(w3 == m[None], NEG, w3)
    return rows


def _stack_rows(rows, n_groups):
    sub = lax.broadcasted_iota(jnp.int32, (8, LANES), 0)
    groups = []
    for gidx in range(n_groups):
        out = jnp.full((8, LANES), NEG, F32)
        for k in range(gidx * 8, min(len(rows), gidx * 8 + 8)):
            out = jnp.where(sub == k - gidx * 8, rows[k], out)
        groups.append(out)
    return groups


def _route_chunk(s1, s2):
    s1v = s1.reshape(N_KEYS // 8, 8, LANES)
    s2v = s2.reshape(N_KEYS // 8, 8, LANES)
    r1 = _top_values(s1v, N_LIST)
    r2 = _top_values(s2v, N_LIST)
    t1 = _stack_rows(r1, LIST_ROWS // 8)
    t2 = _stack_rows(r2, LIST_ROWS // 8)
    slabs = [g + r1[0] for g in t2]
    slabs += [t2[0] + r1[a] for a in range(1, 8)]
    slabs += [g + r2[0] for g in t1[1:]]
    sums = _top_values(jnp.stack(slabs, axis=0), N_LIST)
    z = jnp.zeros_like(sums[0])
    for k in range(TOPK):
        z = z + jnp.exp(sums[k] - sums[0])
    tau = 0.5 * (sums[TOPK - 1] + sums[TOPK])
    scale = (0.5 / z)[None]
    e1 = jnp.exp(s1v - r1[0][None])
    e2 = jnp.exp(s2v - r2[0][None]) * scale
    th = jnp.exp((tau - r2[0])[None] - s1v) * scale
    shape = (N_KEYS, LANES)
    return th.reshape(shape), e1.reshape(shape), e2.reshape(shape)


def _route_kernel(x_ref, g_ref, wq_ref, k_ref, h2t_ref, th_ref, e1_ref, e2_ref, h2_ref, *, tc, n_eb):
    @pl.when(pl.program_id(1) == 0)
    def _():
        h2 = _rms(x_ref[...], g_ref[...])
        h2_ref[...] = h2.astype(BF16)
        h2t_ref[0] = h2.T.astype(BF16)

    q = jnp.dot(h2_ref[...], wq_ref[...], preferred_element_type=F32)
    k1 = k_ref[0, 0]
    k2 = k_ref[0, 1]
    dims = (((1,), (1,)), ((), ()))
    for c in range(tc // LANES):
        qc = q[c * LANES:(c + 1) * LANES]
        s1 = lax.dot_general(k1, qc[:, 0:HALF_DIM], dims, precision=lax.Precision.HIGHEST,
                             preferred_element_type=F32)
        s2 = lax.dot_general(k2, qc[:, HALF_DIM:QUERY_DIM], dims, precision=lax.Precision.HIGHEST,
                             preferred_element_type=F32)
        th, e1, e2 = _route_chunk(s1, s2)
        th_ref[0, :, 0, c] = th.reshape(n_eb, N_KEYS // n_eb, LANES)
        e1_ref[0, :, 0, c] = e1.reshape(n_eb, N_KEYS // n_eb, LANES)
        e2_ref[0, c] = e2


def _route(x1, g, w_q_bf, sub_keys, tc=TOKEN_TILE, eb=EXPERT_BLOCK):
    n_chunks = SEQ // LANES
    nc = tc // LANES
    n_eb = N_EXPERTS // eb
    ib = eb // N_KEYS
    e2_shape = jax.ShapeDtypeStruct((PEER_HEADS, n_chunks, N_KEYS, LANES), F32)
    e2_spec = pl.BlockSpec((1, nc, N_KEYS, LANES), lambda i, h: (h, i, 0, 0))
    row_shape = jax.ShapeDtypeStruct((SEQ // tc, n_eb, PEER_HEADS, nc, ib, LANES), F32)
    row_spec = pl.BlockSpec((1, n_eb, 1, nc, ib, LANES), lambda i, h: (i, 0, h, 0, 0, 0))
    return pl.pallas_call(
        functools.partial(_route_kernel, tc=tc, n_eb=n_eb),
        grid=(SEQ // tc, PEER_HEADS),
        in_specs=[
            pl.BlockSpec((tc, D_MODEL), lambda i, h: (i, 0)),
            pl.BlockSpec((1, D_MODEL), lambda i, h: (0, 0)),
            pl.BlockSpec((D_MODEL, QUERY_DIM), lambda i, h: (0, h)),
            pl.BlockSpec((1, 2, N_KEYS, HALF_DIM), lambda i, h: (h, 0, 0, 0)),
        ],
        out_specs=[pl.BlockSpec((1, D_MODEL, tc), lambda i, h: (i, 0, 0)), row_spec, row_spec, e2_spec],
        out_shape=[jax.ShapeDtypeStruct((SEQ // tc, D_MODEL, tc), BF16), row_shape, row_shape, e2_shape],
        scratch_shapes=[pltpu.VMEM((tc, D_MODEL), BF16)],
        compiler_params=pltpu.CompilerParams(
            dimension_semantics=("arbitrary", "arbitrary"), vmem_limit_bytes=VMEM_LIMIT),
        name="route",
    )(x1, g, w_q_bf, sub_keys)


GELU_C = 0.7978845608028654
GELU_K = 0.044715


def _gated_act_tile(a, gate):
    t = (a * a) * (GELU_C * GELU_K) + GELU_C
    return gate * (a + a * jnp.tanh(a * t))


def _experts_step(u_ref, vt_ref, xt_ref, th_ref, e1_ref, e2_ref, y_ref, p_w, p_r, *, tt, eb):
    def gate_tile(ib, c):
        gate = jnp.zeros((N_KEYS, LANES), F32)
        for h in range(PEER_HEADS):
            th = th_ref[0, 0, h, c, ib:ib + 1, :]
            e1 = e1_ref[0, 0, h, c, ib:ib + 1, :]
            e2 = e2_ref[h, c]
            gate = gate + jnp.where(e2 >= th, e2, 0.0) * e1
        return gate

    n_rb = eb // MXU_DIM
    n_nb = tt // MXU_DIM
    d_rows = D_MODEL // n_rb
    sub = MXU_DIM // N_KEYS
    for rb in range(n_rb):
        rows = slice(rb * MXU_DIM, (rb + 1) * MXU_DIM)
        drows = slice(rb * d_rows, (rb + 1) * d_rows)
        for nb in range(n_nb):
            cols = slice(nb * MXU_DIM, (nb + 1) * MXU_DIM)
            act = jnp.dot(u_ref[rows, :], xt_ref[0, :, cols], preferred_element_type=F32)
            for si in range(sub):
                for sc in range(sub):
                    a = act[si * N_KEYS:(si + 1) * N_KEYS, sc * LANES:(sc + 1) * LANES]
                    gate = gate_tile(rb * sub + si, nb * sub + sc)
                    p_w[rb * MXU_DIM + si * N_KEYS:rb * MXU_DIM + (si + 1) * N_KEYS,
                        nb * MXU_DIM + sc * LANES:nb * MXU_DIM + (sc + 1) * LANES] = (
                            _gated_act_tile(a, gate).astype(BF16))
            y_ref[0, drows, cols] += jnp.dot(vt_ref[0, drows, :], p_r[:, cols], preferred_element_type=F32)


def _experts_kernel(u_ref, vt_ref, xt_ref, th_ref, e1_ref, e2_ref, y_ref, p0, p1, *, tt, eb, n_eb, n_steps):
    g = pl.program_id(0)
    acc_step = jnp.clip(g - 1, 0, n_steps - 1)

    @pl.when(g == 0)
    def _():
        p1[...] = jnp.zeros_like(p1)

    @pl.when(acc_step % n_eb == 0)
    def _():
        y_ref[...] = jnp.zeros_like(y_ref)

    step = functools.partial(_experts_step, u_ref, vt_ref, xt_ref, th_ref, e1_ref, e2_ref, y_ref, tt=tt, eb=eb)

    @pl.when(g % 2 == 0)
    def _():
        step(p0, p1)

    @pl.when(g % 2 == 1)
    def _():
        step(p1, p0)


def _experts(u_bf, vt_bf, h2t, th, e1, e2, tt=TOKEN_TILE, eb=EXPERT_BLOCK):
    ib = eb // N_KEYS
    nc = tt // LANES
    n_eb = N_EXPERTS // eb
    n_steps = (SEQ // tt) * n_eb
    s_act = lambda g: jnp.minimum(g, n_steps - 1)
    s_acc = lambda g: jnp.clip(g - 1, 0, n_steps - 1)
    row_spec = pl.BlockSpec((1, 1, PEER_HEADS, nc, ib, LANES),
                            lambda g: (s_act(g) // n_eb, s_act(g) % n_eb, 0, 0, 0, 0))
    full_spec = pl.BlockSpec((PEER_HEADS, nc, N_KEYS, LANES), lambda g: (0, s_act(g) // n_eb, 0, 0))
    return pl.pallas_call(
        functools.partial(_experts_kernel, tt=tt, eb=eb, n_eb=n_eb, n_steps=n_steps),
        grid=(n_steps + 1,),
        in_specs=[
            pl.BlockSpec((eb, D_MODEL), lambda g: (s_act(g) % n_eb, 0)),
            pl.BlockSpec((1, D_MODEL, eb), lambda g: (s_acc(g) % n_eb, 0, 0)),
            pl.BlockSpec((1, D_MODEL, tt), lambda g: (s_act(g) // n_eb, 0, 0)),
            row_spec, row_spec, full_spec,
        ],
        out_specs=pl.BlockSpec((1, D_MODEL, tt), lambda g: (s_acc(g) // n_eb, 0, 0)),
        out_shape=jax.ShapeDtypeStruct((SEQ // tt, D_MODEL, tt), F32),
        scratch_shapes=[pltpu.VMEM((eb, tt), BF16), pltpu.VMEM((eb, tt), BF16)],
        compiler_params=pltpu.CompilerParams(
            dimension_semantics=("arbitrary",), vmem_limit_bytes=VMEM_LIMIT),
        name="experts",
    )(u_bf, vt_bf, h2t, th, e1, e2)


def _final_kernel(x_ref, yt_ref, g_ref, o_ref):
    o_ref[...] = _rms(x_ref[...] + yt_ref[0].T, g_ref[...])


def _final(x1, y_t, g, tm=TOKEN_TILE):
    return pl.pallas_call(
        _final_kernel,
        grid=(SEQ // tm,),
        in_specs=[
            pl.BlockSpec((tm, D_MODEL), lambda i: (i, 0)),
            pl.BlockSpec((1, D_MODEL, tm), lambda i: (i, 0, 0)),
            pl.BlockSpec((1, D_MODEL), lambda i: (0, 0)),
        ],
        out_specs=pl.BlockSpec((tm, D_MODEL), lambda i: (i, 0)),
        out_shape=jax.ShapeDtypeStruct((SEQ, D_MODEL), F32),
        compiler_params=pltpu.CompilerParams(
            dimension_semantics=("arbitrary",), vmem_limit_bytes=VMEM_LIMIT),
        name="final",
    )(x1, y_t, g)


def kernel(x, norm1_g, w_in, b_gate, pool_scale, w_pool, ln_v_g, ln_v_b, w_s, b_s, w_gproj, w_out,
           norm2_g, w_q, sub_keys, expert_u, expert_v, final_g):
    assert x.shape == (1, SEQ, D_MODEL)
    x2 = x.reshape(SEQ, D_MODEL)
    row = lambda v: v.reshape(1, -1)

    z = _in_proj(x2, row(norm1_g), w_in.astype(BF16))
    x1 = _mixer(z, x2, row(b_gate), row(pool_scale), w_pool.astype(BF16), row(ln_v_g), row(ln_v_b),
                w_s, b_s.T, w_gproj.astype(BF16), w_out.astype(BF16))
    h2t, th, e1, e2 = _route(x1, row(norm2_g), w_q.astype(BF16), sub_keys)
    vt = expert_v.reshape(N_EXPERTS // EXPERT_BLOCK, EXPERT_BLOCK, D_MODEL).transpose(0, 2, 1).astype(BF16)
    y_t = _experts(expert_u.astype(BF16), vt, h2t, th, e1, e2)
    out = _final(x1, y_t, row(final_g))
    return out.reshape(1, SEQ, D_MODEL)
```

```python
import functools

import jax
import jax.numpy as jnp
from jax import lax
from jax.experimental import pallas as pl
from jax.experimental.pallas import tpu as pltpu

D_MODEL = 2048
SEQ = 8192
POOL_WINDOWS = (2, 4, 8, 16)
POOL_WIDTH = 1024
POOL_GROUP_WIDTH = 256
POOL_OUT_GROUP = 512
GMLP_HEADS = 8
GMLP_HEAD_DIM = 128
GMLP_WIDTH = 1024
CHUNK = 128
IN_WIDTH = POOL_WIDTH + 2 * GMLP_WIDTH + 2 * D_MODEL
PEER_HEADS = 8
N_KEYS = 128
N_EXPERTS = N_KEYS * N_KEYS
HALF_DIM = 128
QUERY_DIM = 256
TOPK = 16
EPS = 1e-6

LANES = 128
MXU_DIM = 256
TOKEN_TILE = 512
EXPERT_BLOCK = 1024
EXPERT_TOKENS = 512
ROUTE_HEADS = 4
MAX_WINDOW = max(POOL_WINDOWS)
NEG = -1e30
VMEM_LIMIT = 56 * 1024 * 1024

F32 = jnp.float32
BF16 = jnp.bfloat16


def _rms(x, g):
    return x * lax.rsqrt(jnp.mean(x * x, axis=-1, keepdims=True) + EPS) * g


def _in_proj_kernel(x_ref, g_ref, w_ref, z_ref, h_ref):
    @pl.when(pl.program_id(1) == 0)
    def _():
        h_ref[...] = _rms(x_ref[...], g_ref[...]).astype(BF16)

    z_ref[...] = jnp.dot(h_ref[...], w_ref[...], preferred_element_type=F32).astype(z_ref.dtype)


def _in_proj(x, g, w_in_bf, tm=1024, tn=1024):
    return pl.pallas_call(
        _in_proj_kernel,
        grid=(SEQ // tm, IN_WIDTH // tn),
        in_specs=[
            pl.BlockSpec((tm, D_MODEL), lambda i, j: (i, 0)),
            pl.BlockSpec((1, D_MODEL), lambda i, j: (0, 0)),
            pl.BlockSpec((D_MODEL, tn), lambda i, j: (0, j)),
        ],
        out_specs=pl.BlockSpec((tm, tn), lambda i, j: (i, j)),
        out_shape=jax.ShapeDtypeStruct((SEQ, IN_WIDTH), BF16),
        scratch_shapes=[pltpu.VMEM((tm, D_MODEL), BF16)],
        compiler_params=pltpu.CompilerParams(
            dimension_semantics=("arbitrary", "arbitrary"), vmem_limit_bytes=VMEM_LIMIT),
        name="in_proj",
    )(x, g, w_in_bf)


def _mixer_kernel(z_ref, x_ref, bg_ref, ps_ref, wp_ref, lng_ref, lnb_ref, ws_ref, bst_ref,
                  wg_ref, wo_ref, o_ref, halo_ref, ext_ref, ysgu_ref, ypool_ref, *, tm):
    i = pl.program_id(0)

    @pl.when(i == 0)
    def _():
        halo_ref[...] = jnp.zeros_like(halo_ref)

    a = z_ref[:, 0:POOL_WIDTH].astype(F32)
    ext_ref[0:MAX_WINDOW, :] = halo_ref[...]
    ext_ref[MAX_WINDOW:MAX_WINDOW + tm, :] = a
    halo_ref[...] = a[tm - MAX_WINDOW:tm, :]
    pos1 = (i * tm + 1 + lax.broadcasted_iota(jnp.int32, (tm, POOL_GROUP_WIDTH), 0)).astype(F32)
    for gi, w in enumerate(POOL_WINDOWS):
        lo = gi * POOL_GROUP_WIDTH
        hi = lo + POOL_GROUP_WIDTH
        win = ext_ref[MAX_WINDOW:MAX_WINDOW + tm, lo:hi]
        for k in range(1, w):
            win = win + ext_ref[MAX_WINDOW - k:MAX_WINDOW - k + tm, lo:hi]
        cnt = jnp.minimum(pos1, float(w))
        pooled = (win / cnt - ext_ref[MAX_WINDOW:MAX_WINDOW + tm, lo:hi]) * ps_ref[:, lo:hi]
        ypool_ref[:, gi * POOL_OUT_GROUP:(gi + 1) * POOL_OUT_GROUP] = jnp.dot(
            pooled.astype(BF16), wp_ref[gi], preferred_element_type=F32)

    u = jax.nn.gelu(z_ref[:, POOL_WIDTH:POOL_WIDTH + GMLP_WIDTH].astype(F32))
    vg = jax.nn.gelu(z_ref[:, POOL_WIDTH + GMLP_WIDTH:POOL_WIDTH + 2 * GMLP_WIDTH].astype(F32))
    mu = jnp.mean(vg, axis=-1, keepdims=True)
    vc = vg - mu
    var = jnp.mean(vc * vc, axis=-1, keepdims=True)
    vn = (vc * lax.rsqrt(var + EPS) * lng_ref[...] + lnb_ref[...]).astype(BF16)
    row = lax.broadcasted_iota(jnp.int32, (CHUNK, CHUNK), 0)
    col = lax.broadcasted_iota(jnp.int32, (CHUNK, CHUNK), 1)
    for h in range(GMLP_HEADS):
        wsm = jnp.where(row >= col, ws_ref[h], 0.0).astype(BF16)
        bias = bst_ref[:, h:h + 1]
        cs = slice(h * GMLP_HEAD_DIM, (h + 1) * GMLP_HEAD_DIM)
        for c in range(tm // CHUNK):
            rs = slice(c * CHUNK, (c + 1) * CHUNK)
            sv = jnp.dot(wsm, vn[rs, cs], preferred_element_type=F32) + bias
            ysgu_ref[rs, cs] = (u[rs, cs] * sv).astype(BF16)
    y_sgu = jnp.dot(ysgu_ref[...], wg_ref[...], preferred_element_type=F32)

    g_off = POOL_WIDTH + 2 * GMLP_WIDTH
    ga = jax.nn.sigmoid(z_ref[:, g_off:g_off + D_MODEL].astype(F32) + bg_ref[:, 0:D_MODEL])
    gb = jax.nn.sigmoid(z_ref[:, g_off + D_MODEL:g_off + 2 * D_MODEL].astype(F32)
                        + bg_ref[:, D_MODEL:2 * D_MODEL])
    merged = (ga * ypool_ref[...] + gb * y_sgu).astype(BF16)
    o_ref[...] = x_ref[...] + jnp.dot(merged, wo_ref[...], preferred_element_type=F32)


def _mixer(z, x, b_gate, pool_scale, w_pool_bf, ln_g, ln_b, w_s, b_s_t, w_gproj_bf, w_out_bf, tm=256):
    const = lambda *shape: pl.BlockSpec(shape, lambda i: (0,) * len(shape), pipeline_mode=pl.Buffered(1))
    return pl.pallas_call(
        functools.partial(_mixer_kernel, tm=tm),
        grid=(SEQ // tm,),
        in_specs=[
            pl.BlockSpec((tm, IN_WIDTH), lambda i: (i, 0)),
            pl.BlockSpec((tm, D_MODEL), lambda i: (i, 0)),
            const(1, 2 * D_MODEL),
            const(1, POOL_WIDTH),
            const(len(POOL_WINDOWS), POOL_GROUP_WIDTH, POOL_OUT_GROUP),
            const(1, GMLP_WIDTH),
            const(1, GMLP_WIDTH),
            const(GMLP_HEADS, CHUNK, CHUNK),
            const(CHUNK, GMLP_HEADS),
            const(GMLP_WIDTH, D_MODEL),
            const(D_MODEL, D_MODEL),
        ],
        out_specs=pl.BlockSpec((tm, D_MODEL), lambda i: (i, 0)),
        out_shape=jax.ShapeDtypeStruct((SEQ, D_MODEL), F32),
        scratch_shapes=[
            pltpu.VMEM((MAX_WINDOW, POOL_WIDTH), F32),
            pltpu.VMEM((MAX_WINDOW + tm, POOL_WIDTH), F32),
            pltpu.VMEM((tm, GMLP_WIDTH), BF16),
            pltpu.VMEM((tm, D_MODEL), F32),
        ],
        compiler_params=pltpu.CompilerParams(
            dimension_semantics=("arbitrary",), vmem_limit_bytes=VMEM_LIMIT),
        name="mixer",
    )(z, x, b_gate, pool_scale, w_pool_bf, ln_g, ln_b, w_s, b_s_t, w_gproj_bf, w_out_bf)


def _sorting_network(n):
    pairs = []

    def merge(lo, hi, r):
        step = r * 2
        if step < hi - lo:
            merge(lo, hi, step)
            merge(lo + r, hi, step)
            pairs.extend((i, i + r) for i in range(lo + r, hi - r, step))
        else:
            pairs.append((lo, lo + r))

    def sort(lo, hi):
        if hi > lo:
            mid = lo + (hi - lo) // 2
            sort(lo, mid)
            sort(mid + 1, hi)
            merge(lo, hi, 1)

    sort(0, n - 1)
    return tuple(pairs)


SORT_TOPK = _sorting_network(TOPK)


def _compare_exchange(vs, i, j):
    vs[i], vs[j] = jnp.maximum(vs[i], vs[j]), jnp.minimum(vs[i], vs[j])


def _sort_desc(vs):
    vs = list(vs)
    for i, j in SORT_TOPK:
        _compare_exchange(vs, i, j)
    return vs


def _bitonic_sort_desc(vs):
    vs = list(vs)
    stride = len(vs) // 2
    while stride:
        for k in range(len(vs)):
            if not k & stride:
                _compare_exchange(vs, k, k + stride)
        stride //= 2
    return vs


def _merge_sublanes(vs):
    for shift in (4, 2, 1):
        other = [pltpu.roll(v, shift, axis=0) for v in vs]
        vs = _bitonic_sort_desc([jnp.maximum(vs[k], other[TOPK - 1 - k]) for k in range(TOPK)])
    return vs


def _max_sublanes(m):
    for shift in (4, 2, 1):
        m = jnp.maximum(m, pltpu.roll(m, shift, axis=0))
    return m


def _below(x, lim):
    return jnp.where(x < lim, x, NEG)


def _top_and_next(sv):
    top = _merge_sublanes(_sort_desc([sv[k] for k in range(N_KEYS // 8)]))
    nxt = _max_sublanes(jnp.max(_below(sv, top[TOPK - 1][None]), axis=0))
    return top, nxt


def _route_chunk(s1, s2):
    assert N_KEYS // 8 == TOPK
    s1v = s1.reshape(N_KEYS // 8, 8, LANES)
    s2v = s2.reshape(N_KEYS // 8, 8, LANES)
    d1, next1 = _top_and_next(s1v)
    d2, next2 = _top_and_next(s2v)
    sub = lax.broadcasted_iota(jnp.int32, (8, LANES), 0)
    t1_lo, t1_hi = d1[0], d1[8]
    for a in range(1, 8):
        t1_lo = jnp.where(sub == a, d1[a], t1_lo)
        t1_hi = jnp.where(sub == a, d1[8 + a], t1_hi)
    cand = [t1_lo + d2[b] for b in range(TOPK)]
    edge = t1_hi + d2[0]
    sums = _merge_sublanes(_bitonic_sort_desc(cand[:TOPK - 1] + [jnp.maximum(cand[TOPK - 1], edge)]))
    lim = sums[TOPK - 1]
    runner = _below(edge, lim)
    for x in cand:
        runner = jnp.maximum(runner, _below(x, lim))
    runner = _max_sublanes(runner)
    for x in (d1[0] + next2, next1 + d2[0]):
        runner = jnp.maximum(runner, _below(x, lim))
    z = jnp.zeros_like(lim)
    for k in range(TOPK):
        z = z + jnp.exp(sums[k] - sums[0])
    tau = 0.5 * (lim + runner)
    scale = (0.5 / z)[None]
    e1 = jnp.exp(s1v - d1[0][None])
    e2 = jnp.exp(s2v - d2[0][None]) * scale
    th = jnp.exp((tau - d2[0])[None] - s1v) * scale
    shape = (N_KEYS, LANES)
    return th.reshape(shape), e1.reshape(shape), e2.reshape(shape)


def _route_kernel(x_ref, g_ref, wq_ref, k_ref, h2t_ref, th_ref, e1_ref, e2_ref, h2_ref, *, tc, n_eb):
    @pl.when(pl.program_id(1) == 0)
    def _():
        h2 = _rms(x_ref[...], g_ref[...])
        h2_ref[...] = h2.astype(BF16)
        h2t_ref[0] = h2.T.astype(BF16)

    q = jnp.dot(h2_ref[...], wq_ref[...], preferred_element_type=F32)
    dims = (((1,), (1,)), ((), ()))
    for hh in range(ROUTE_HEADS):
        k1 = k_ref[hh, 0]
        k2 = k_ref[hh, 1]
        for c in range(tc // LANES):
            qc = q[c * LANES:(c + 1) * LANES, hh * QUERY_DIM:(hh + 1) * QUERY_DIM]
            s1 = lax.dot_general(k1, qc[:, 0:HALF_DIM], dims, precision=lax.Precision.HIGHEST,
                                 preferred_element_type=F32)
            s2 = lax.dot_general(k2, qc[:, HALF_DIM:QUERY_DIM], dims, precision=lax.Precision.HIGHEST,
                                 preferred_element_type=F32)
            th, e1, e2 = _route_chunk(s1, s2)
            th_ref[0, :, hh, c] = th.reshape(n_eb, N_KEYS // n_eb, LANES)
            e1_ref[0, :, hh, c] = e1.reshape(n_eb, N_KEYS // n_eb, LANES)
            e2_ref[hh, c] = e2


def _route(x1, g, w_q_bf, sub_keys, tc=TOKEN_TILE, eb=EXPERT_BLOCK):
    n_chunks = SEQ // LANES
    nc = tc // LANES
    n_eb = N_EXPERTS // eb
    ib = eb // N_KEYS
    e2_shape = jax.ShapeDtypeStruct((PEER_HEADS, n_chunks, N_KEYS, LANES), F32)
    e2_spec = pl.BlockSpec((ROUTE_HEADS, nc, N_KEYS, LANES), lambda i, h: (h, i, 0, 0))
    row_shape = jax.ShapeDtypeStruct((SEQ // tc, n_eb, PEER_HEADS, nc, ib, LANES), F32)
    row_spec = pl.BlockSpec((1, n_eb, ROUTE_HEADS, nc, ib, LANES), lambda i, h: (i, 0, h, 0, 0, 0))
    return pl.pallas_call(
        functools.partial(_route_kernel, tc=tc, n_eb=n_eb),
        grid=(SEQ // tc, PEER_HEADS // ROUTE_HEADS),
        in_specs=[
            pl.BlockSpec((tc, D_MODEL), lambda i, h: (i, 0)),
            pl.BlockSpec((1, D_MODEL), lambda i, h: (0, 0)),
            pl.BlockSpec((D_MODEL, ROUTE_HEADS * QUERY_DIM), lambda i, h: (0, h)),
            pl.BlockSpec((ROUTE_HEADS, 2, N_KEYS, HALF_DIM), lambda i, h: (h, 0, 0, 0)),
        ],
        out_specs=[pl.BlockSpec((1, D_MODEL, tc), lambda i, h: (i, 0, 0)), row_spec, row_spec, e2_spec],
        out_shape=[jax.ShapeDtypeStruct((SEQ // tc, D_MODEL, tc), BF16), row_shape, row_shape, e2_shape],
        scratch_shapes=[pltpu.VMEM((tc, D_MODEL), BF16)],
        compiler_params=pltpu.CompilerParams(
            dimension_semantics=("arbitrary", "arbitrary"), vmem_limit_bytes=VMEM_LIMIT),
        name="route",
    )(x1, g, w_q_bf, sub_keys)


GELU_C = 0.7978845608028654
GELU_K = 0.044715


def _gated_act_tile(a, gate):
    t = (a * a) * (GELU_C * GELU_K) + GELU_C
    return gate * (a + a * jnp.tanh(a * t))


def _experts_step(u_ref, vt_ref, xt_ref, th_ref, e1_ref, e2_ref, y_ref, p_w, p_r, *, tt, eb):
    chunks_per_tile = TOKEN_TILE // LANES

    def gate_tile(ib, c):
        t, cc = divmod(c, chunks_per_tile)
        gate = jnp.zeros((N_KEYS, LANES), F32)
        for h in range(PEER_HEADS):
            th = th_ref[t, 0, h, cc, ib:ib + 1, :]
            e1 = e1_ref[t, 0, h, cc, ib:ib + 1, :]
            e2 = e2_ref[h, c]
            gate = gate + jnp.where(e2 >= th, e2, 0.0) * e1
        return gate

    n_rb = eb // MXU_DIM
    n_nb = tt // MXU_DIM
    d_rows = D_MODEL // n_rb
    sub = MXU_DIM // N_KEYS
    for rb in range(n_rb):
        rows = slice(rb * MXU_DIM, (rb + 1) * MXU_DIM)
        drows = slice(rb * d_rows, (rb + 1) * d_rows)
        for nb in range(n_nb):
            cols = slice(nb * MXU_DIM, (nb + 1) * MXU_DIM)
            t, off = divmod(nb * MXU_DIM, TOKEN_TILE)
            act = jnp.dot(u_ref[rows, :], xt_ref[t, :, off:off + MXU_DIM], preferred_element_type=F32)
            for si in range(sub):
                for sc in range(sub):
                    a = act[si * N_KEYS:(si + 1) * N_KEYS, sc * LANES:(sc + 1) * LANES]
                    gate = gate_tile(rb * sub + si, nb * sub + sc)
                    p_w[rb * MXU_DIM + si * N_KEYS:rb * MXU_DIM + (si + 1) * N_KEYS,
                        nb * MXU_DIM + sc * LANES:nb * MXU_DIM + (sc + 1) * LANES] = (
                            _gated_act_tile(a, gate).astype(BF16))
            y_ref[drows, cols] += jnp.dot(vt_ref[0, drows, :], p_r[:, cols], preferred_element_type=F32)


def _experts_kernel(u_ref, vt_ref, xt_ref, th_ref, e1_ref, e2_ref, x1_ref, fg_ref, o_ref, y_ref, p0, p1,
                    *, tt, eb, n_eb, n_steps):
    g = pl.program_id(0)
    acc_step = jnp.clip(g - 1, 0, n_steps - 1)

    @pl.when(g == 0)
    def _():
        p1[...] = jnp.zeros_like(p1)

    @pl.when(acc_step % n_eb == 0)
    def _():
        y_ref[...] = jnp.zeros_like(y_ref)

    step = functools.partial(_experts_step, u_ref, vt_ref, xt_ref, th_ref, e1_ref, e2_ref, y_ref, tt=tt, eb=eb)

    @pl.when(g % 2 == 0)
    def _():
        step(p0, p1)

    @pl.when(g % 2 == 1)
    def _():
        step(p1, p0)

    @pl.when((acc_step % n_eb == n_eb - 1) & (g > 0))
    def _():
        o_ref[...] = _rms(x1_ref[...] + y_ref[...].T, fg_ref[...])


def _experts(u_bf, vt_bf, h2t, th, e1, e2, x1, final_g, tt=EXPERT_TOKENS, eb=EXPERT_BLOCK):
    ib = eb // N_KEYS
    nc = tt // LANES
    n_eb = N_EXPERTS // eb
    n_steps = (SEQ // tt) * n_eb
    tiles = tt // TOKEN_TILE
    s_act = lambda g: jnp.minimum(g, n_steps - 1)
    s_acc = lambda g: jnp.clip(g - 1, 0, n_steps - 1)
    once = pl.Buffered(1)
    row_spec = pl.BlockSpec((tiles, 1, PEER_HEADS, TOKEN_TILE // LANES, ib, LANES),
                            lambda g: (s_act(g) // n_eb, s_act(g) % n_eb, 0, 0, 0, 0))
    full_spec = pl.BlockSpec((PEER_HEADS, nc, N_KEYS, LANES), lambda g: (0, s_act(g) // n_eb, 0, 0),
                             pipeline_mode=once)
    return pl.pallas_call(
        functools.partial(_experts_kernel, tt=tt, eb=eb, n_eb=n_eb, n_steps=n_steps),
        grid=(n_steps + 1,),
        in_specs=[
            pl.BlockSpec((eb, D_MODEL), lambda g: (s_act(g) % n_eb, 0)),
            pl.BlockSpec((1, D_MODEL, eb), lambda g: (s_acc(g) % n_eb, 0, 0)),
            pl.BlockSpec((tiles, D_MODEL, TOKEN_TILE), lambda g: (s_act(g) // n_eb, 0, 0), pipeline_mode=once),
            row_spec, row_spec, full_spec,
            pl.BlockSpec((tt, D_MODEL), lambda g: (s_acc(g) // n_eb, 0)),
            pl.BlockSpec((1, D_MODEL), lambda g: (0, 0)),
        ],
        out_specs=pl.BlockSpec((tt, D_MODEL), lambda g: (s_acc(g) // n_eb, 0)),
        out_shape=jax.ShapeDtypeStruct((SEQ, D_MODEL), F32),
        scratch_shapes=[pltpu.VMEM((D_MODEL, tt), F32), pltpu.VMEM((eb, tt), BF16), pltpu.VMEM((eb, tt), BF16)],
        compiler_params=pltpu.CompilerParams(
            dimension_semantics=("arbitrary",), vmem_limit_bytes=VMEM_LIMIT),
        name="experts",
    )(u_bf, vt_bf, h2t, th, e1, e2, x1, final_g)


def kernel(x, norm1_g, w_in, b_gate, pool_scale, w_pool, ln_v_g, ln_v_b, w_s, b_s, w_gproj, w_out,
           norm2_g, w_q, sub_keys, expert_u, expert_v, final_g):
    assert x.shape == (1, SEQ, D_MODEL)
    x2 = x.reshape(SEQ, D_MODEL)
    row = lambda v: v.reshape(1, -1)

    z = _in_proj(x2, row(norm1_g), w_in.astype(BF16))
    x1 = _mixer(z, x2, row(b_gate), row(pool_scale), w_pool.astype(BF16), row(ln_v_g), row(ln_v_b),
                w_s, b_s.T, w_gproj.astype(BF16), w_out.astype(BF16))
    h2t, th, e1, e2 = _route(x1, row(norm2_g), w_q.astype(BF16), sub_keys)
    vt = expert_v.reshape(N_EXPERTS // EXPERT_BLOCK, EXPERT_BLOCK, D_MODEL).transpose(0, 2, 1).astype(BF16)
    out = _experts(expert_u.astype(BF16), vt, h2t, th, e1, e2, x1, row(final_g))
    return out.reshape(1, SEQ, D_MODEL)
```
